```python
import math
import jax, jax.numpy as jnp
from jax import lax
import numpy as np

D_MODEL = 1024
BATCH = 8
SEQ = 8192
DEPTH = 1

PLE_DIM = 256
CHUNK = 128
EPS = 1e-6
E_A = D_MODEL
G_A = 4
D_INNER = 2 * D_MODEL
HEAD_DIM = 64
N_HEADS = D_INNER // HEAD_DIM
N_STATE = 128
N_GROUPS = 4
CONV_K = 4
CONV_DIM = D_INNER + 2 * N_GROUPS * N_STATE
COL_SIZES = (E_A, E_A, E_A, D_INNER, CONV_DIM, N_HEADS, D_MODEL, D_MODEL)
N_IN = sum(COL_SIZES)

kernel_name = 'hybrid_gmlp_ssd_gated_merge_ple'


def rms_norm(x, g):
    xf = x.astype(jnp.float32)
    y = xf * lax.rsqrt(jnp.mean(xf * xf, axis=-1, keepdims=True) + EPS)
    return (y * g.astype(jnp.float32)).astype(x.dtype)


def layer_norm(x, g, b):
    xf = x.astype(jnp.float32)
    mu = jnp.mean(xf, axis=-1, keepdims=True)
    xc = xf - mu
    y = xc * lax.rsqrt(jnp.mean(xc * xc, axis=-1, keepdims=True) + EPS)
    return (y * g.astype(jnp.float32) + b.astype(jnp.float32)).astype(x.dtype)


def gmlp_branch(u, v, z, ln_g, ln_b, w_s, b_s):
    bsz, s, e = u.shape
    nc = s // CHUNK
    u = jax.nn.gelu(u)
    v = layer_norm(jax.nn.gelu(v), ln_g, ln_b)
    mask = jnp.tril(jnp.ones((CHUNK, CHUNK), dtype=bool))
    ws = jnp.where(mask[None], w_s, jnp.zeros_like(w_s)).astype(v.dtype)
    vc = v.reshape(bsz, nc, CHUNK, G_A, e // G_A)
    sv = jnp.einsum('gts,bcsgd->bctgd', ws, vc) + b_s.T.astype(v.dtype)[None, None, :, :, None]
    return u * sv.reshape(bsz, s, e) * jax.nn.silu(z)


def causal_depthwise_conv(x, w, b):
    c = x.shape[-1]
    y = lax.conv_general_dilated(x, w.astype(x.dtype)[:, None, :], window_strides=(1,),
                                 padding=[(CONV_K - 1, 0)],
                                 dimension_numbers=('NWC', 'WIO', 'NWC'),
                                 feature_group_count=c)
    return y + b.astype(x.dtype)


def ssd_scan(xs, dt, a_log, bm, cm, d_skip):
    bsz, s, h, pdim = xs.shape
    nc = s // CHUNK
    g = N_GROUPS
    r = h // g
    dtype = xs.dtype
    a = -jnp.exp(a_log.astype(jnp.float32)).reshape(g, r)
    X = xs.reshape(bsz, nc, CHUNK, g, r, pdim)
    dtc = dt.reshape(bsz, nc, CHUNK, g, r)
    Xdt = X * dtc[..., None].astype(dtype)
    dA_cs = jnp.cumsum(dtc * a, axis=2)
    Bc = bm.reshape(bsz, nc, CHUNK, g, N_STATE)
    Cc = cm.reshape(bsz, nc, CHUNK, g, N_STATE)
    mask = jnp.tril(jnp.ones((CHUNK, CHUNK), dtype=bool))[None, None, :, :, None, None]
    seg = dA_cs[:, :, :, None] - dA_cs[:, :, None, :]
    Lmat = jnp.exp(jnp.where(mask, seg, -jnp.inf)).astype(dtype)
    CB = jnp.einsum('bclgn,bcsgn->bclsg', Cc, Bc)
    y_diag = jnp.einsum('bclsg,bclsgr,bcsgrp->bclgrp', CB, Lmat, Xdt)
    decay_s = jnp.exp(dA_cs[:, :, -1:] - dA_cs).astype(dtype)
    states = jnp.einsum('bcsgn,bcsgr,bcsgrp->bcgrpn', Bc, decay_s, Xdt)
    chunk_decay = jnp.exp(dA_cs[:, :, -1]).astype(dtype)

    def step(hstate, inp):
        st, dec = inp
        return dec[..., None, None] * hstate + st, hstate

    h0 = jnp.zeros((bsz, g, r, pdim, N_STATE), dtype=states.dtype)
    _, prev = lax.scan(step, h0, (jnp.moveaxis(states, 1, 0), jnp.moveaxis(chunk_decay, 1, 0)))
    prev = jnp.moveaxis(prev, 0, 1)
    y_off = jnp.einsum('bclgn,bcgrpn,bclgr->bclgrp', Cc, prev, jnp.exp(dA_cs).astype(dtype))
    y = y_diag + y_off + X * d_skip.astype(dtype).reshape(g, r)[..., None]
    return y.reshape(bsz, s, h * pdim)


def gated_group_rms_norm(y, z, g):
    yz = y * jax.nn.silu(z)
    bsz, s, e = yz.shape
    yg = yz.reshape(bsz, s, N_GROUPS, e // N_GROUPS).astype(jnp.float32)
    yg = yg * lax.rsqrt(jnp.mean(yg * yg, axis=-1, keepdims=True) + EPS)
    return (yg.reshape(bsz, s, e) * g.astype(jnp.float32)).astype(y.dtype)


def setup_inputs(seed: int = 0) -> dict:
    key = jax.random.key(seed)
    ks = jax.random.split(key, 24)
    f32 = jnp.float32
    nrm = lambda k, shape, scale: jax.random.normal(k, shape, f32) * scale
    gain = lambda k, shape: 1.0 + 0.02 * jax.random.normal(k, shape, f32)
    dt0 = jnp.exp(jax.random.uniform(ks[10], (DEPTH, N_HEADS), f32) * (math.log(0.1) - math.log(0.001)) + math.log(0.001))
    return {
        'x': jax.random.normal(ks[0], (BATCH, SEQ, D_MODEL), f32),
        'p': jax.random.normal(ks[1], (DEPTH, BATCH, SEQ, PLE_DIM), f32),
        'norm_g': gain(ks[2], (DEPTH, D_MODEL)),
        'w_in': nrm(ks[3], (DEPTH, D_MODEL, N_IN), D_MODEL ** -0.5),
        'ln_a_g': gain(ks[4], (DEPTH, E_A)),
        'ln_a_b': nrm(ks[5], (DEPTH, E_A), 0.02),
        'w_s': nrm(ks[6], (DEPTH, G_A, CHUNK, CHUNK), CHUNK ** -0.5),
        'b_s': gain(ks[7], (DEPTH, G_A, CHUNK)),
        'conv_w': nrm(ks[8], (DEPTH, CONV_K, CONV_DIM), CONV_K ** -0.5),
        'conv_b': nrm(ks[9], (DEPTH, CONV_DIM), 0.02),
        'dt_bias': dt0 + jnp.log(-jnp.expm1(-dt0)),
        'a_log': jnp.log(jax.random.uniform(ks[11], (DEPTH, N_HEADS), f32, 1.0, 16.0)),
        'd_skip': gain(ks[12], (DEPTH, N_HEADS)),
        'ssm_norm_g': gain(ks[13], (DEPTH, D_INNER)),
        'w_oa': nrm(ks[14], (DEPTH, E_A, D_MODEL), E_A ** -0.5),
        'w_ob': nrm(ks[15], (DEPTH, D_INNER, D_MODEL), D_INNER ** -0.5),
        'w_out': nrm(ks[16], (DEPTH, D_MODEL, D_MODEL), D_MODEL ** -0.5),
        'ple_norm_g': gain(ks[17], (DEPTH, D_MODEL)),
        'w_pg': nrm(ks[18], (DEPTH, D_MODEL, D_MODEL), D_MODEL ** -0.5),
        'w_ple': nrm(ks[19], (DEPTH, PLE_DIM, D_MODEL), PLE_DIM ** -0.5),
        'final_g': gain(ks[20], (D_MODEL,)),
    }


def reference(x, p, norm_g, w_in, ln_a_g, ln_a_b, w_s, b_s, conv_w, conv_b, dt_bias, a_log,
              d_skip, ssm_norm_g, w_oa, w_ob, w_out, ple_norm_g, w_pg, w_ple, final_g):
    bsz, s, _ = x.shape
    splits = list(np.cumsum(COL_SIZES)[:-1])
    for i in range(DEPTH):
        h = rms_norm(x, norm_g[i])
        proj = h @ w_in[i].astype(h.dtype)
        u, v, z_a, z_b, xbc, dt_raw, g_a, g_b = jnp.split(proj, splits, axis=-1)
        y_a = gmlp_branch(u, v, z_a, ln_a_g[i], ln_a_b[i], w_s[i], b_s[i])
        o_a = y_a @ w_oa[i].astype(y_a.dtype)
        xbc = jax.nn.silu(causal_depthwise_conv(xbc, conv_w[i], conv_b[i]))
        xs, bm, cm = jnp.split(xbc, [D_INNER, D_INNER + N_GROUPS * N_STATE], axis=-1)
        dt = jax.nn.softplus(dt_raw.astype(jnp.float32) + dt_bias[i].astype(jnp.float32))
        y = ssd_scan(xs.reshape(bsz, s, N_HEADS, HEAD_DIM), dt, a_log[i],
                     bm.reshape(bsz, s, N_GROUPS, N_STATE), cm.reshape(bsz, s, N_GROUPS, N_STATE), d_skip[i])
        y_b = gated_group_rms_norm(y, z_b, ssm_norm_g[i])
        o_b = y_b @ w_ob[i].astype(y_b.dtype)
        merged = jax.nn.sigmoid(g_a) * o_a + jax.nn.sigmoid(g_b) * o_b
        x = x + merged @ w_out[i].astype(merged.dtype)
        hp = rms_norm(x, ple_norm_g[i])
        x = x + jax.nn.sigmoid(hp @ w_pg[i].astype(hp.dtype)) * (p[i] @ w_ple[i].astype(p.dtype))
    return rms_norm(x, final_g)
```

```python
import functools
import math

import jax
import jax.numpy as jnp
from jax import lax
from jax.experimental import pallas as pl
from jax.experimental.pallas import tpu as pltpu

D_MODEL = 1024
PLE_DIM = 256
CHUNK = 128
EPS = 1e-6
E_A = D_MODEL
G_A = 4
D_INNER = 2 * D_MODEL
HEAD_DIM = 64
N_HEADS = D_INNER // HEAD_DIM
N_STATE = 128
N_GROUPS = 4
HEADS_PER_GROUP = N_HEADS // N_GROUPS
CONV_K = 4
CONV_DIM = D_INNER + 2 * N_GROUPS * N_STATE
LANES = 128
SUBLANES = 8

OFF_A = 0
OFF_ZB = OFF_A + 3 * E_A
OFF_XBC = OFF_ZB + D_INNER
OFF_G = OFF_XBC + CONV_DIM
OFF_DT = OFF_G + 2 * D_MODEL
N_PACKED = OFF_DT + LANES

TILE_S = 256
VMEM_LIMIT_BYTES = 58 * 1024 * 1024

F32 = jnp.float32
BF16 = jnp.bfloat16


def _dot(a, b):
    return jnp.dot(a, b, preferred_element_type=F32)


def _dot_nt(a, b):
    return lax.dot_general(a, b, (((1,), (1,)), ((), ())), preferred_element_type=F32)


def _sigmoid(x):
    return 1.0 / (1.0 + jnp.exp(-x))


def _silu(x):
    return x * _sigmoid(x)


def _softplus(x):
    return jnp.maximum(x, 0.0) + jnp.log1p(jnp.exp(-jnp.abs(x)))


def _rms_norm(x, g):
    return x * lax.rsqrt(jnp.mean(x * x, axis=-1, keepdims=True) + EPS) * g


def _split3(v, lane):
    hi = v.astype(BF16).astype(F32)
    r1 = v - hi
    mid = r1.astype(BF16).astype(F32)
    lo = (r1 - mid).astype(BF16).astype(F32)
    out = jnp.where(lane < N_HEADS, hi, 0.0)
    out = jnp.where((lane >= N_HEADS) & (lane < 2 * N_HEADS), pltpu.roll(mid, N_HEADS, 1), out)
    out = jnp.where((lane >= 2 * N_HEADS) & (lane < 3 * N_HEADS), pltpu.roll(lo, 2 * N_HEADS, 1), out)
    return out.astype(BF16)


def _layer_kernel(
    x_ref, p_ref, w_ref, w_oa_ref, w_ob_ref, w_out_ref, w_pg_ref, w_ple_ref,
    ws_ref, bs_ref, conv_w_ref, conv_b_ref, dt_bias_ref, a_log_ref, d_skip_ref,
    norm_g_ref, ln_g_ref, ln_b_ref, ssm_g_ref, ple_g_ref, final_g_ref,
    tril_ref, e64_ref, e128_ref,
    o_ref,
    xpad_ref, y_ref, state_ref,
):
    ts = x_ref.shape[1]
    n_chunks = ts // CHUNK

    @pl.when(pl.program_id(1) == 0)
    def _():
        state_ref[...] = jnp.zeros_like(state_ref)
        xpad_ref[0:SUBLANES, :] = jnp.zeros((SUBLANES, CONV_DIM), F32)

    x = x_ref[0]
    hb = _rms_norm(x, norm_g_ref[...]).astype(BF16)

    pa = _dot(hb, w_ref[:, OFF_A:OFF_A + 3 * E_A])
    u = jax.nn.gelu(pa[:, 0:E_A])
    v = jax.nn.gelu(pa[:, E_A:2 * E_A])
    mu = jnp.mean(v, axis=-1, keepdims=True)
    vc = v - mu
    v = vc * lax.rsqrt(jnp.mean(vc * vc, axis=-1, keepdims=True) + EPS) * ln_g_ref[...] + ln_b_ref[...]
    vb = v.astype(BF16)
    ga_w = E_A // G_A
    sv_rows = []
    for c in range(n_chunks):
        cols = []
        for g in range(G_A):
            cols.append(_dot(ws_ref[g], vb[c * CHUNK:(c + 1) * CHUNK, g * ga_w:(g + 1) * ga_w]))
        sv_rows.append(jnp.concatenate(cols, axis=1) + bs_ref[...])
    sv = jnp.concatenate(sv_rows, axis=0)
    ya = (u * sv * _silu(pa[:, 2 * E_A:3 * E_A])).astype(BF16)
    oa = _dot(ya, w_oa_ref[...])

    xr = _dot(hb, w_ref[:, OFF_XBC:OFF_XBC + CONV_DIM])
    xpad_ref[SUBLANES:SUBLANES + ts, :] = xr
    xc = conv_b_ref[...] + conv_w_ref[CONV_K - 1:CONV_K, :] * xr
    for k in range(CONV_K - 1):
        shift = CONV_K - 1 - k
        xc = xc + conv_w_ref[k:k + 1, :] * xpad_ref[SUBLANES - shift:SUBLANES - shift + ts, :]
    xpad_ref[0:SUBLANES, :] = xr[ts - SUBLANES:ts, :]
    xbc = _silu(xc)

    dt = _softplus(_dot(hb, w_ref[:, OFF_DT:OFF_DT + LANES]) + dt_bias_ref[...])
    a_row = -jnp.exp(a_log_ref[...])

    lane = lax.broadcasted_iota(jnp.int32, (CHUNK, LANES), 1)
    row = lax.broadcasted_iota(jnp.int32, (CHUNK, LANES), 0)
    causal = row >= lane
    left_half = lane < HEAD_DIM
    gw = HEADS_PER_GROUP * HEAD_DIM

    for c in range(n_chunks):
        r0 = c * CHUNK
        xs_c = xbc[r0:r0 + CHUNK, 0:D_INNER]
        bm_c = xbc[r0:r0 + CHUNK, D_INNER:D_INNER + N_GROUPS * N_STATE]
        cm_c = xbc[r0:r0 + CHUNK, D_INNER + N_GROUPS * N_STATE:CONV_DIM]
        dt_c = dt[r0:r0 + CHUNK, :]
        da = dt_c * a_row
        cs3 = _dot(tril_ref[...], _split3(da, lane))
        acs = cs3 + pltpu.roll(cs3, LANES - N_HEADS, 1) + pltpu.roll(cs3, LANES - 2 * N_HEADS, 1)
        acs_last = acs[CHUNK - 1:CHUNK, :]
        e_in = jnp.exp(acs)
        w_in = dt_c * jnp.exp(acs_last - acs)
        ew = _dot(jnp.concatenate([_split3(e_in, lane), _split3(w_in, lane)], axis=0), e64_ref[...])
        e_x = ew[0:CHUNK, :]
        w_x = ew[CHUNK:2 * CHUNK, :]
        acs_t = acs.T
        dt_t = dt_c.T
        acs3 = _split3(acs, lane)
        xs_b = xs_c.astype(BF16)
        xw_b = (xs_c * w_x).astype(BF16)
        for g in range(N_GROUPS):
            cg = cm_c[:, g * N_STATE:(g + 1) * N_STATE].astype(BF16)
            bg = bm_c[:, g * N_STATE:(g + 1) * N_STATE]
            cb = _dot_nt(cg, bg.astype(BF16))
            a_col = _dot(acs3, e128_ref[:, g * HEADS_PER_GROUP * LANES:(g + 1) * HEADS_PER_GROUP * LANES])
            prev_g = state_ref[:, g * gw:(g + 1) * gw]
            y_g = _dot(cg, prev_g.astype(BF16)) * e_x[:, g * gw:(g + 1) * gw]
            pair_out = []
            for pr in range(HEADS_PER_GROUP // 2):
                m_pair = []
                for q in range(2):
                    hl = 2 * pr + q
                    h = g * HEADS_PER_GROUP + hl
                    seg = a_col[:, hl * LANES:(hl + 1) * LANES] - acs_t[h:h + 1, :]
                    decay = jnp.exp(jnp.where(causal, seg, -jnp.inf))
                    m_pair.append((cb * decay * dt_t[h:h + 1, :]).astype(BF16))
                x_pair = xs_b[:, (g * HEADS_PER_GROUP + 2 * pr) * HEAD_DIM:(g * HEADS_PER_GROUP + 2 * pr + 2) * HEAD_DIM]
                zero = jnp.zeros_like(x_pair)
                rhs = jnp.concatenate([jnp.where(left_half, x_pair, zero), jnp.where(left_half, zero, x_pair)], axis=0)
                pair_out.append(_dot(jnp.concatenate(m_pair, axis=1), rhs))
            y_g = y_g + jnp.concatenate(pair_out, axis=1)
            y_ref[r0:r0 + CHUNK, g * gw:(g + 1) * gw] = (
                y_g + xs_c[:, g * gw:(g + 1) * gw] * d_skip_ref[:, g * gw:(g + 1) * gw])
            s_new = _dot(bg.T.astype(BF16), xw_b[:, g * gw:(g + 1) * gw])
            state_ref[:, g * gw:(g + 1) * gw] = prev_g * e_x[CHUNK - 1:CHUNK, g * gw:(g + 1) * gw] + s_new

    zb = _dot(hb, w_ref[:, OFF_ZB:OFF_ZB + D_INNER])
    yz = y_ref[...] * _silu(zb)
    yn = []
    for g in range(N_GROUPS):
        yg = yz[:, g * gw:(g + 1) * gw]
        yn.append(yg * lax.rsqrt(jnp.mean(yg * yg, axis=-1, keepdims=True) + EPS))
    yb = (jnp.concatenate(yn, axis=1) * ssm_g_ref[...]).astype(BF16)
    ob = _dot(yb, w_ob_ref[...])
    gates = _sigmoid(_dot(hb, w_ref[:, OFF_G:OFF_G + 2 * D_MODEL]))
    merged = (gates[:, 0:D_MODEL] * oa + gates[:, D_MODEL:2 * D_MODEL] * ob).astype(BF16)
    x1 = x + _dot(merged, w_out_ref[...])

    hp = _rms_norm(x1, ple_g_ref[...]).astype(BF16)
    x2 = x1 + _sigmoid(_dot(hp, w_pg_ref[...])) * _dot(p_ref[0].astype(BF16), w_ple_ref[...])
    o_ref[0] = _rms_norm(x2, final_g_ref[...])


def _resident(shape):
    zeros = (0,) * len(shape)
    return pl.BlockSpec(shape, lambda b, j: zeros, pipeline_mode=pl.Buffered(1))


def _head_expander(width):
    r = jnp.arange(LANES)[:, None]
    c = jnp.arange(N_HEADS * width)[None, :]
    return ((r < 3 * N_HEADS) & ((r % N_HEADS) == (c // width))).astype(BF16)


def kernel(x, p, norm_g, w_in, ln_a_g, ln_a_b, w_s, b_s, conv_w, conv_b, dt_bias, a_log, d_skip, ssm_norm_g, w_oa, w_ob, w_out, ple_norm_g, w_pg, w_ple, final_g):
    bsz, s, d = x.shape
    assert d == D_MODEL and s % TILE_S == 0 and TILE_S % CHUNK == 0
    assert w_in.shape[0] == 1, "single-layer stack"
    i = 0
    w = w_in[i]
    c0 = 3 * E_A
    c1 = c0 + D_INNER
    c2 = c1 + CONV_DIM
    c3 = c2 + N_HEADS
    w_packed = jnp.concatenate(
        [w[:, 0:c0], w[:, c0:c1], w[:, c1:c2], w[:, c3:c3 + 2 * D_MODEL],
         jnp.pad(w[:, c2:c3], ((0, 0), (0, LANES - N_HEADS)))], axis=1).astype(BF16)
    tril = jnp.tril(jnp.ones((CHUNK, CHUNK), F32))
    ws = jnp.where(tril[None] > 0, w_s[i], 0.0).astype(BF16)
    bs_full = jnp.repeat(b_s[i].T, E_A // G_A, axis=1)
    pad_heads = lambda t: jnp.pad(t[i][None, :], ((0, 0), (0, LANES - N_HEADS)))
    row = lambda t: t.reshape(1, -1)
    operands = [
        x, p[i], w_packed, w_oa[i].astype(BF16), w_ob[i].astype(BF16), w_out[i].astype(BF16),
        w_pg[i].astype(BF16), w_ple[i].astype(BF16), ws, bs_full, conv_w[i], row(conv_b[i]),
        pad_heads(dt_bias), pad_heads(a_log), jnp.repeat(d_skip[i], HEAD_DIM)[None, :],
        row(norm_g[i]), row(ln_a_g[i]), row(ln_a_b[i]), row(ssm_norm_g[i]), row(ple_norm_g[i]),
        row(final_g), tril.astype(BF16), _head_expander(HEAD_DIM), _head_expander(LANES),
    ]
    in_specs = [
        pl.BlockSpec((1, TILE_S, D_MODEL), lambda b, j: (b, j, 0)),
        pl.BlockSpec((1, TILE_S, PLE_DIM), lambda b, j: (b, j, 0)),
    ] + [_resident(op.shape) for op in operands[2:]]
    return pl.pallas_call(
        _layer_kernel,
        grid=(bsz, s // TILE_S),
        in_specs=in_specs,
        out_specs=pl.BlockSpec((1, TILE_S, D_MODEL), lambda b, j: (b, j, 0)),
        out_shape=jax.ShapeDtypeStruct(x.shape, x.dtype),
        scratch_shapes=[
            pltpu.VMEM((TILE_S + SUBLANES, CONV_DIM), F32),
            pltpu.VMEM((TILE_S, D_INNER), F32),
            pltpu.VMEM((N_STATE, D_INNER), F32),
        ],
        compiler_params=pltpu.CompilerParams(
            dimension_semantics=("arbitrary", "arbitrary"),
            vmem_limit_bytes=VMEM_LIMIT_BYTES,
        ),
        name="hybrid_layer",
    )(*operands)
```

```python
import jax
import jax.numpy as jnp
from jax import lax
from jax.experimental import pallas as pl
from jax.experimental.pallas import tpu as pltpu

D_MODEL = 1024
PLE_DIM = 256
CHUNK = 128
EPS = 1e-6
E_A = D_MODEL
G_A = 4
D_INNER = 2 * D_MODEL
HEAD_DIM = 64
N_HEADS = D_INNER // HEAD_DIM
N_STATE = 128
N_GROUPS = 4
HEADS_PER_GROUP = N_HEADS // N_GROUPS
GROUP_W = HEADS_PER_GROUP * HEAD_DIM
CONV_K = 4
CONV_DIM = D_INNER + 2 * N_GROUPS * N_STATE
CONV_BLOCKS = CONV_DIM // 128
LANES = 128
SUBLANES = 8

OFF_A = 0
OFF_ZB = OFF_A + 3 * E_A
OFF_XBC = OFF_ZB + D_INNER
OFF_G = OFF_XBC + CONV_DIM
OFF_DT = OFF_G + 2 * D_MODEL
N_PACKED = OFF_DT + LANES

TILE_S = 256
SLICE_N = 512
VMEM_LIMIT_BYTES = 62 * 1024 * 1024

F32 = jnp.float32
BF16 = jnp.bfloat16


def _dot(a, b):
    return jnp.dot(a, b, preferred_element_type=F32)


def _dot_nt(a, b):
    return lax.dot_general(a, b, (((1,), (1,)), ((), ())), preferred_element_type=F32)


def _sigmoid(x):
    return 1.0 / (1.0 + jnp.exp(-x))


def _silu(x):
    return x * _sigmoid(x)


def _softplus(x):
    return jnp.maximum(x, 0.0) + jnp.log1p(jnp.exp(-jnp.abs(x)))


def _rms_norm(x, g):
    return x * lax.rsqrt(jnp.mean(x * x, axis=-1, keepdims=True) + EPS) * g


def _split3(v, lane):
    hi = v.astype(BF16).astype(F32)
    r1 = v - hi
    mid = r1.astype(BF16).astype(F32)
    lo = (r1 - mid).astype(BF16).astype(F32)
    out = jnp.where(lane < N_HEADS, hi, 0.0)
    out = jnp.where((lane >= N_HEADS) & (lane < 2 * N_HEADS), pltpu.roll(mid, N_HEADS, 1), out)
    out = jnp.where((lane >= 2 * N_HEADS) & (lane < 3 * N_HEADS), pltpu.roll(lo, 2 * N_HEADS, 1), out)
    return out.astype(BF16)


def _expand_heads(v, lane):
    half = lane // HEAD_DIM
    blocks = [jnp.take_along_axis(v, 2 * pr + half, axis=1) for pr in range(N_HEADS // 2)]
    return jnp.concatenate(blocks, axis=1)


def _layer_kernel(
    x_ref, p_ref, w_ref, w_oa_ref, w_ob_ref, w_out_ref, w_pg_ref, w_ple_ref,
    ws_ref, bs_ref, conv_w_ref, conv_b_ref, dt_bias_ref, a_log_ref, d_skip_ref,
    norm_g_ref, ln_g_ref, ln_b_ref, ssm_g_ref, ple_g_ref, final_g_ref,
    tril_ref,
    o_ref,
    xpad_ref, xbc_ref, y_ref, state_ref,
):
    ts = x_ref.shape[1]
    n_chunks = ts // CHUNK

    @pl.when(pl.program_id(1) == 0)
    def _():
        state_ref[...] = jnp.zeros_like(state_ref)
        xpad_ref[:, 0:SUBLANES, :] = jnp.zeros((CONV_BLOCKS, SUBLANES, LANES), F32)

    x = x_ref[0]
    hb = _rms_norm(x, norm_g_ref[...]).astype(BF16)

    def proj(off):
        return _dot(hb, w_ref[:, off:off + SLICE_N])

    lane = lax.broadcasted_iota(jnp.int32, (CHUNK, LANES), 1)
    row = lax.broadcasted_iota(jnp.int32, (CHUNK, LANES), 0)
    causal = row >= lane
    left_half = lane < HEAD_DIM

    dt = _softplus(_dot(hb, w_ref[:, OFF_DT:OFF_DT + LANES]) + dt_bias_ref[...])
    a_row = -jnp.exp(a_log_ref[...])

    def chunk_decays(c):
        dt_c = dt[c * CHUNK:(c + 1) * CHUNK, :]
        cs3 = _dot(tril_ref[...], _split3(dt_c * a_row, lane))
        acs = cs3 + pltpu.roll(cs3, LANES - N_HEADS, 1) + pltpu.roll(cs3, LANES - 2 * N_HEADS, 1)
        acs_last = acs[CHUNK - 1:CHUNK, :]
        return dict(
            acs=acs, acs_t=acs.T, dt_t=dt_c.T,
            e_x=_expand_heads(jnp.exp(acs), lane),
            w_x=_expand_heads(dt_c * jnp.exp(acs_last - acs), lane))

    decays = [chunk_decays(c) for c in range(n_chunks)]

    def conv_slice(sl, xr):
        for bl in range(SLICE_N // LANES):
            cblk = sl * (SLICE_N // LANES) + bl
            cs = slice(cblk * LANES, (cblk + 1) * LANES)
            xr_c = xr[:, bl * LANES:(bl + 1) * LANES]
            xpad_ref[cblk, SUBLANES:SUBLANES + ts, :] = xr_c
            xc = conv_b_ref[:, cs] + conv_w_ref[CONV_K - 1:CONV_K, cs] * xr_c
            for k in range(CONV_K - 1):
                shift = CONV_K - 1 - k
                xc = xc + conv_w_ref[k:k + 1, cs] * xpad_ref[cblk, SUBLANES - shift:SUBLANES - shift + ts, :]
            xpad_ref[cblk, 0:SUBLANES, :] = xr_c[ts - SUBLANES:ts, :]
            xbc_ref[:, cs] = _silu(xc)

    n_xbc = CONV_DIM // SLICE_N
    xr_prev = proj(OFF_XBC)
    for sl in range(1, n_xbc):
        xr = proj(OFF_XBC + sl * SLICE_N)
        conv_slice(sl - 1, xr_prev)
        xr_prev = xr
    pa = [proj(OFF_A)]
    conv_slice(n_xbc - 1, xr_prev)
    pa += [proj(OFF_A + sl * SLICE_N) for sl in range(1, 3 * E_A // SLICE_N)]
    pa = jnp.concatenate(pa, axis=1)

    zb = [_silu(proj(OFF_ZB))]
    u = jax.nn.gelu(pa[:, 0:E_A])
    zb.append(_silu(proj(OFF_ZB + SLICE_N)))
    v = jax.nn.gelu(pa[:, E_A:2 * E_A])
    vc = v - jnp.mean(v, axis=-1, keepdims=True)
    v = vc * lax.rsqrt(jnp.mean(vc * vc, axis=-1, keepdims=True) + EPS) * ln_g_ref[...] + ln_b_ref[...]
    vb = v.astype(BF16)
    zb.append(_silu(proj(OFF_ZB + 2 * SLICE_N)))
    zb.append(_silu(proj(OFF_ZB + 3 * SLICE_N)))
    zb = jnp.concatenate(zb, axis=1)

    side = {}
    ga_w = E_A // G_A

    def gate_slice(k):
        def run():
            side.setdefault("gates", []).append(_sigmoid(proj(OFF_G + k * SLICE_N)))
        return run

    def pos_mix(c):
        def run():
            cols = [_dot(ws_ref[g], vb[c * CHUNK:(c + 1) * CHUNK, g * ga_w:(g + 1) * ga_w]) for g in range(G_A)]
            side.setdefault("sv", []).append(jnp.concatenate(cols, axis=1) + bs_ref[...])
        return run

    def out_a(k):
        def run():
            if "ya" not in side:
                sv = jnp.concatenate(side["sv"], axis=0)
                side["ya"] = (u * sv * _silu(pa[:, 2 * E_A:3 * E_A])).astype(BF16)
            side.setdefault("oa", []).append(_dot(side["ya"], w_oa_ref[:, k * SLICE_N:(k + 1) * SLICE_N]))
        return run

    fillers = ([gate_slice(k) for k in range(2 * D_MODEL // SLICE_N)]
               + [pos_mix(c) for c in range(n_chunks)] + [out_a(k) for k in range(D_MODEL // SLICE_N)])

    for c in range(n_chunks):
        r0 = c * CHUNK
        dec = decays[c]
        acs, acs_t, dt_t, e_x, w_x = dec["acs"], dec["acs_t"], dec["dt_t"], dec["e_x"], dec["w_x"]
        for g in range(N_GROUPS):
            gs = slice(g * GROUP_W, (g + 1) * GROUP_W)
            if fillers:
                fillers.pop(0)()
            xs_g = xbc_ref[r0:r0 + CHUNK, gs]
            bg = xbc_ref[r0:r0 + CHUNK, D_INNER + g * N_STATE:D_INNER + (g + 1) * N_STATE]
            cg = xbc_ref[r0:r0 + CHUNK, D_INNER + (N_GROUPS + g) * N_STATE:D_INNER + (N_GROUPS + g + 1) * N_STATE]
            cg = cg.astype(BF16)
            xs_b = xs_g.astype(BF16)
            cb = _dot_nt(cg, bg.astype(BF16))
            prev_g = state_ref[:, gs]
            y_g = _dot(cg, prev_g.astype(BF16)) * e_x[:, gs]
            pair_out = []
            for pr in range(HEADS_PER_GROUP // 2):
                m_pair = []
                for q in range(2):
                    h = g * HEADS_PER_GROUP + 2 * pr + q
                    seg = acs[:, h:h + 1] - acs_t[h:h + 1, :]
                    decay = jnp.exp(jnp.where(causal, seg, -jnp.inf))
                    m_pair.append((cb * decay * dt_t[h:h + 1, :]).astype(BF16))
                x_pair = xs_b[:, 2 * pr * HEAD_DIM:(2 * pr + 2) * HEAD_DIM]
                zero = jnp.zeros_like(x_pair)
                rhs = jnp.concatenate([jnp.where(left_half, x_pair, zero), jnp.where(left_half, zero, x_pair)], axis=0)
                pair_out.append(_dot(jnp.concatenate(m_pair, axis=1), rhs))
            y_g = y_g + jnp.concatenate(pair_out, axis=1)
            y_ref[r0:r0 + CHUNK, gs] = y_g + xs_g * d_skip_ref[:, gs]
            s_new = _dot(bg.T.astype(BF16), (xs_g * w_x[:, gs]).astype(BF16))
            state_ref[:, gs] = prev_g * e_x[CHUNK - 1:CHUNK, gs] + s_new
    while fillers:
        fillers.pop(0)()

    pe = _dot(p_ref[0].astype(BF16), w_ple_ref[...])
    yz = y_ref[...] * zb
    yn = []
    for g in range(N_GROUPS):
        yg = yz[:, g * GROUP_W:(g + 1) * GROUP_W]
        yn.append(yg * lax.rsqrt(jnp.mean(yg * yg, axis=-1, keepdims=True) + EPS))
    yb = (jnp.concatenate(yn, axis=1) * ssm_g_ref[...]).astype(BF16)
    ob = _dot(yb, w_ob_ref[...])
    gates = jnp.concatenate(side["gates"], axis=1)
    oa = jnp.concatenate(side["oa"], axis=1)
    merged = (gates[:, 0:D_MODEL] * oa + gates[:, D_MODEL:2 * D_MODEL] * ob).astype(BF16)
    x1 = x + _dot(merged, w_out_ref[...])

    hp = _rms_norm(x1, ple_g_ref[...]).astype(BF16)
    x2 = x1 + _sigmoid(_dot(hp, w_pg_ref[...])) * pe
    o_ref[0] = _rms_norm(x2, final_g_ref[...])


def _resident(shape):
    zeros = (0,) * len(shape)
    return pl.BlockSpec(shape, lambda b, j: zeros, pipeline_mode=pl.Buffered(1))


def kernel(x, p, norm_g, w_in, ln_a_g, ln_a_b, w_s, b_s, conv_w, conv_b, dt_bias, a_log, d_skip, ssm_norm_g, w_oa, w_ob, w_out, ple_norm_g, w_pg, w_ple, final_g):
    bsz, s, d = x.shape
    assert d == D_MODEL and s % TILE_S == 0 and TILE_S % CHUNK == 0
    assert w_in.shape[0] == 1, "single-layer stack"
    i = 0
    w = w_in[i]
    c0 = 3 * E_A
    c1 = c0 + D_INNER
    c2 = c1 + CONV_DIM
    c3 = c2 + N_HEADS
    w_packed = jnp.concatenate(
        [w[:, 0:c0], w[:, c0:c1], w[:, c1:c2], w[:, c3:c3 + 2 * D_MODEL],
         jnp.pad(w[:, c2:c3], ((0, 0), (0, LANES - N_HEADS)))], axis=1).astype(BF16)
    tril = jnp.tril(jnp.ones((CHUNK, CHUNK), F32))
    ws = jnp.where(tril[None] > 0, w_s[i], 0.0).astype(BF16)
    bs_full = jnp.repeat(b_s[i].T, E_A // G_A, axis=1)
    pad_heads = lambda t: jnp.pad(t[i][None, :], ((0, 0), (0, LANES - N_HEADS)))
    row = lambda t: t.reshape(1, -1)
    operands = [
        x, p[i], w_packed, w_oa[i].astype(BF16), w_ob[i].astype(BF16), w_out[i].astype(BF16),
        w_pg[i].astype(BF16), w_ple[i].astype(BF16), ws, bs_full, conv_w[i], row(conv_b[i]),
        pad_heads(dt_bias), pad_heads(a_log), jnp.repeat(d_skip[i], HEAD_DIM)[None, :],
        row(norm_g[i]), row(ln_a_g[i]), row(ln_a_b[i]), row(ssm_norm_g[i]), row(ple_norm_g[i]),
        row(final_g), tril.astype(BF16),
    ]
    in_specs = [
        pl.BlockSpec((1, TILE_S, D_MODEL), lambda b, j: (b, j, 0)),
        pl.BlockSpec((1, TILE_S, PLE_DIM), lambda b, j: (b, j, 0)),
    ] + [_resident(op.shape) for op in operands[2:]]
    return pl.pallas_call(
        _layer_kernel,
        grid=(bsz, s // TILE_S),
        in_specs=in_specs,
        out_specs=pl.BlockSpec((1, TILE_S, D_MODEL), lambda b, j: (b, j, 0)),
        out_shape=jax.ShapeDtypeStruct(x.shape, x.dtype),
        scratch_shapes=[
            pltpu.VMEM((CONV_BLOCKS, TILE_S + SUBLANES, LANES), F32),
            pltpu.VMEM((TILE_S, CONV_DIM), F32),
            pltpu.VMEM((TILE_S, D_INNER), F32),
            pltpu.VMEM((N_STATE, D_INNER), F32),
        ],
        compiler_params=pltpu.CompilerParams(
            dimension_semantics=("arbitrary", "arbitrary"),
            vmem_limit_bytes=VMEM_LIMIT_BYTES,
        ),
        name="hybrid_layer",
    )(*operands)
```
